```python
import math
import jax, jax.numpy as jnp
from jax import lax
import numpy as np

D_MODEL = 4096
BATCH = 8
SEQ = 2048
DEPTH = 4
DEC_BATCH = 2
DEC_SEQ = 8192
PAST_LEN = 128

GRID_W = 64
BRANCH_W = D_MODEL // 2
NA_HEADS = 16
NA_HEAD_DIM = BRANCH_W // NA_HEADS
NA_KR_MAX = 8
NA_KC = 16
DA_HEADS = 16
DA_V_DIM = BRANCH_W // DA_HEADS
DA_QK_DIM = DA_V_DIM // 2
ROPE_THETA = 500000.0
ROPE_DIM = DA_QK_DIM // 4
Q_BLOCK = 128
EPS = 1e-6
IN_W = 8 * BRANCH_W + 2 * D_MODEL
SPLITS = [BRANCH_W * i for i in range(1, 9)]

kernel_name = "hybrid_natten_diffattn_gated_encoder"


def rms_norm(x, g):
    xf = x.astype(jnp.float32)
    y = xf * lax.rsqrt(jnp.mean(xf * xf, axis=-1, keepdims=True) + EPS)
    return (y * g.astype(jnp.float32)).astype(x.dtype)


def rope_partial(x, pos):
    half = ROPE_DIM // 2
    inv = ROPE_THETA ** (-jnp.arange(0, ROPE_DIM, 2, dtype=jnp.float32) / ROPE_DIM)
    ang = pos.astype(jnp.float32)[:, None] * inv[None, :]
    cos = jnp.cos(ang)[None, :, None, None, :]
    sin = jnp.sin(ang)[None, :, None, None, :]
    xr = x[..., :ROPE_DIM].astype(jnp.float32)
    x1, x2 = xr[..., :half], xr[..., half:]
    rot = jnp.concatenate([x1 * cos - x2 * sin, x2 * cos + x1 * sin], axis=-1).astype(x.dtype)
    return jnp.concatenate([rot, x[..., ROPE_DIM:]], axis=-1)


def neighbourhood_attention(q, k, v, rpb):
    B, S, H, dh = q.shape
    rows = S // GRID_W
    kr = min(NA_KR_MAX, rows)
    qg = q.reshape(B, rows, GRID_W, H, dh)
    kg = k.reshape(B, rows, GRID_W, H, dh)
    vg = v.reshape(B, rows, GRID_W, H, dh)
    r_idx = jnp.arange(rows)
    r_start = jnp.clip(r_idx - kr // 2, 0, rows - kr)
    c_idx = jnp.arange(GRID_W)
    c_start = jnp.clip(c_idx - NA_KC // 2, 0, GRID_W - NA_KC)
    col_ok = (c_idx[None, :] >= c_start[:, None]) & (c_idx[None, :] < c_start[:, None] + NA_KC)
    col_off = jnp.clip(c_idx[None, :] - c_idx[:, None], -(NA_KC - 1), NA_KC - 1) + NA_KC - 1
    scale = dh ** -0.5

    def one_row(args):
        r, rs, q_r = args
        k_r = lax.dynamic_slice_in_dim(kg, rs, kr, axis=1)
        v_r = lax.dynamic_slice_in_dim(vg, rs, kr, axis=1)
        row_off = rs + jnp.arange(kr) - r + NA_KR_MAX - 1
        bias = rpb[:, row_off][:, :, col_off]
        bias = jnp.transpose(bias, (0, 2, 1, 3)).astype(jnp.float32)
        s = jnp.einsum('bqhd,bikhd->bhqik', q_r, k_r, preferred_element_type=jnp.float32) * scale
        s = s + bias[None]
        s = jnp.where(col_ok[None, None, :, None, :], s, -jnp.inf)
        p = jax.nn.softmax(s.reshape(B, H, GRID_W, kr * GRID_W), axis=-1)
        p = p.reshape(B, H, GRID_W, kr, GRID_W).astype(v.dtype)
        return jnp.einsum('bhqik,bikhd->bqhd', p, v_r)

    out = lax.map(one_row, (r_idx, r_start, jnp.moveaxis(qg, 1, 0)))
    return jnp.moveaxis(out, 0, 1).reshape(B, S, H, dh)


def differential_attention(q, k, v, lam, lam_init, subln_g):
    B, S, H, _, dqk = q.shape
    nblk = S // Q_BLOCK
    qb = jnp.moveaxis(q.reshape(B, nblk, Q_BLOCK, H, 2, dqk), 1, 0)
    scale = dqk ** -0.5

    def one_block(q_blk):
        s = jnp.einsum('bqhmd,bkhmd->bhmqk', q_blk, k, preferred_element_type=jnp.float32) * scale
        p = jax.nn.softmax(s, axis=-1)
        a = p[:, :, 0] - lam * p[:, :, 1]
        return jnp.einsum('bhqk,bkhd->bqhd', a.astype(v.dtype), v)

    o = lax.map(one_block, qb)
    o = jnp.moveaxis(o, 0, 1).reshape(B, S, H, -1)
    return rms_norm(o, subln_g) * (1.0 - lam_init)


def encoder_layer(x, g_norm, w_in, b_merge, rpb, lq1, lk1, lq2, lk2, subln_g,
                  w_up_a, w_up_b, w_o, lam_init):
    B, S, _ = x.shape
    h = rms_norm(x, g_norm)
    z = h @ w_in
    a_q, a_k, a_v, a_g, b_q, b_k, b_v, b_g, m = jnp.split(z, SPLITS, axis=-1)

    shp_a = (B, S, NA_HEADS, NA_HEAD_DIM)
    o_a = neighbourhood_attention(a_q.reshape(shp_a), a_k.reshape(shp_a), a_v.reshape(shp_a), rpb)
    p_a = (o_a.reshape(B, S, BRANCH_W) * jax.nn.silu(a_g)) @ w_up_a

    pos = jnp.arange(S)
    qd = rope_partial(b_q.reshape(B, S, DA_HEADS, 2, DA_QK_DIM), pos)
    kd = rope_partial(b_k.reshape(B, S, DA_HEADS, 2, DA_QK_DIM), pos)
    vd = b_v.reshape(B, S, DA_HEADS, DA_V_DIM)
    lam = (jnp.exp(jnp.sum(lq1.astype(jnp.float32) * lk1.astype(jnp.float32)))
           - jnp.exp(jnp.sum(lq2.astype(jnp.float32) * lk2.astype(jnp.float32))) + lam_init)
    o_b = differential_attention(qd, kd, vd, lam, lam_init, subln_g)
    p_b = (o_b.reshape(B, S, BRANCH_W) * jax.nn.silu(b_g)) @ w_up_b

    gates = jax.nn.sigmoid((m + b_merge).astype(jnp.float32)).astype(x.dtype)
    merged = gates[..., :D_MODEL] * p_a + gates[..., D_MODEL:] * p_b
    return x + merged @ w_o


def trunk(x, norm_g, w_in, b_merge, rpb, lambda_q1, lambda_k1, lambda_q2, lambda_k2,
          subln_g, w_up_a, w_up_b, w_o, final_g):
    for l in range(DEPTH):
        lam_init = 0.8 - 0.6 * math.exp(-0.3 * l)
        x = encoder_layer(x, norm_g[l], w_in[l], b_merge[l], rpb[l], lambda_q1[l], lambda_k1[l],
                          lambda_q2[l], lambda_k2[l], subln_g[l], w_up_a[l], w_up_b[l], w_o[l], lam_init)
    return rms_norm(x, final_g)


def setup_inputs(seed: int = 0) -> dict:
    key = jax.random.key(seed)
    ks = jax.random.split(key, 16)
    f = jnp.float32
    return {
        "x_prompt": jax.random.normal(ks[0], (BATCH, SEQ, D_MODEL), f),
        "x_sample": jax.random.normal(ks[1], (DEC_BATCH, DEC_SEQ, D_MODEL), f),
        "norm_g": 1.0 + 0.02 * jax.random.normal(ks[2], (DEPTH, D_MODEL), f),
        "w_in": jax.random.normal(ks[3], (DEPTH, D_MODEL, IN_W), f) * D_MODEL ** -0.5,
        "b_merge": 0.02 * jax.random.normal(ks[4], (DEPTH, 2 * D_MODEL), f),
        "rpb": 0.02 * jax.random.normal(ks[5], (DEPTH, NA_HEADS, 2 * NA_KR_MAX - 1, 2 * NA_KC - 1), f),
        "lambda_q1": 0.1 * jax.random.normal(ks[6], (DEPTH, DA_QK_DIM), f),
        "lambda_k1": 0.1 * jax.random.normal(ks[7], (DEPTH, DA_QK_DIM), f),
        "lambda_q2": 0.1 * jax.random.normal(ks[8], (DEPTH, DA_QK_DIM), f),
        "lambda_k2": 0.1 * jax.random.normal(ks[9], (DEPTH, DA_QK_DIM), f),
        "subln_g": 1.0 + 0.02 * jax.random.normal(ks[10], (DEPTH, DA_V_DIM), f),
        "w_up_a": jax.random.normal(ks[11], (DEPTH, BRANCH_W, D_MODEL), f) * BRANCH_W ** -0.5,
        "w_up_b": jax.random.normal(ks[12], (DEPTH, BRANCH_W, D_MODEL), f) * BRANCH_W ** -0.5,
        "w_o": jax.random.normal(ks[13], (DEPTH, D_MODEL, D_MODEL), f) * D_MODEL ** -0.5,
        "final_g": 1.0 + 0.02 * jax.random.normal(ks[14], (D_MODEL,), f),
    }


def reference(x_prompt, x_sample, norm_g, w_in, b_merge, rpb, lambda_q1, lambda_k1, lambda_q2,
              lambda_k2, subln_g, w_up_a, w_up_b, w_o, final_g):
    y_prompt = trunk(x_prompt, norm_g, w_in, b_merge, rpb, lambda_q1, lambda_k1, lambda_q2,
                     lambda_k2, subln_g, w_up_a, w_up_b, w_o, final_g)
    y_sample = trunk(x_sample, norm_g, w_in, b_merge, rpb, lambda_q1, lambda_k1, lambda_q2,
                     lambda_k2, subln_g, w_up_a, w_up_b, w_o, final_g)
    return (y_prompt, y_sample)
```

```python
import functools
import math

import numpy as np
import jax
import jax.numpy as jnp
from jax import lax
from jax.experimental import pallas as pl
from jax.experimental.pallas import tpu as pltpu

GRID_W = 64
HEAD_DIM = 128
NA_KR = 8
NA_KC = 16
DA_QK_DIM = 64
ROPE_THETA = 500000.0
ROPE_DIM = 16
EPS = 1e-6

VMEM_LIMIT_BYTES = 56 * 1024 * 1024
LANES = 128

NA_ROWS_PER_STEP = 8
NA_WIN_ROWS = NA_ROWS_PER_STEP + NA_KR
NA_SUB_Q = 128

F32 = jnp.float32
BF16 = jnp.bfloat16


def _params(*sem):
    return pltpu.CompilerParams(dimension_semantics=sem, vmem_limit_bytes=VMEM_LIMIT_BYTES)


def _rmsnorm_kernel(x_ref, g_ref, o_ref):
    x = x_ref[...]
    y = x * lax.rsqrt(jnp.mean(x * x, axis=-1, keepdims=True) + EPS)
    o_ref[...] = (y * g_ref[...]).astype(o_ref.dtype)


def _rmsnorm(x, g, out_dtype):
    t, d = x.shape
    tm = min(256, t)
    return pl.pallas_call(
        _rmsnorm_kernel,
        out_shape=jax.ShapeDtypeStruct((t, d), out_dtype),
        grid=(t // tm,),
        in_specs=[pl.BlockSpec((tm, d), lambda i: (i, 0)),
                  pl.BlockSpec((1, d), lambda i: (0, 0))],
        out_specs=pl.BlockSpec((tm, d), lambda i: (i, 0)),
        compiler_params=_params("parallel"),
        name="rmsnorm",
    )(x, g.reshape(1, d))


def _proj_plain_kernel(h_ref, w_ref, o_ref):
    acc = jnp.dot(h_ref[...], w_ref[...], preferred_element_type=F32)
    o_ref[...] = acc.astype(o_ref.dtype)


def _proj_silu_kernel(h_ref, w_ref, o_ref):
    acc = jnp.dot(h_ref[...], w_ref[...], preferred_element_type=F32)
    o_ref[...] = (acc * jax.nn.sigmoid(acc)).astype(o_ref.dtype)


def _proj_sigmoid_kernel(h_ref, w_ref, b_ref, o_ref):
    acc = jnp.dot(h_ref[...], w_ref[...], preferred_element_type=F32)
    o_ref[...] = jax.nn.sigmoid(acc + b_ref[...]).astype(o_ref.dtype)


def _proj_rope_kernel(h_ref, w_ref, c_ref, sa_ref, sb_ref, o_ref):
    acc = jnp.dot(h_ref[...], w_ref[...], preferred_element_type=F32)
    c, sa, sb = c_ref[...], sa_ref[...], sb_ref[...]
    half = ROPE_DIM // 2
    for j in range(acc.shape[1] // LANES):
        z = acc[:, j * LANES:(j + 1) * LANES]
        up = pltpu.roll(z, LANES - half, 1)
        dn = pltpu.roll(z, half, 1)
        o_ref[:, j * LANES:(j + 1) * LANES] = (z * c + up * sa + dn * sb).astype(o_ref.dtype)


def _proj(kernel_fn, h, w, out_dtype, extra=(), extra_specs=(), tm=1024, tn=1024, name="proj"):
    t, k = h.shape
    n = w.shape[1]
    tm, tn = min(tm, t), min(tn, n)
    return pl.pallas_call(
        kernel_fn,
        out_shape=jax.ShapeDtypeStruct((t, n), out_dtype),
        grid=(t // tm, n // tn),
        in_specs=[pl.BlockSpec((tm, k), lambda i, j: (i, 0)),
                  pl.BlockSpec((k, tn), lambda i, j: (0, j))] + list(extra_specs),
        out_specs=pl.BlockSpec((tm, tn), lambda i, j: (i, j)),
        compiler_params=_params("parallel", "parallel"),
        name=name,
    )(h, w, *extra)


def _rope_tables(seq):
    half = ROPE_DIM // 2
    inv = ROPE_THETA ** (-jnp.arange(0, ROPE_DIM, 2, dtype=F32) / ROPE_DIM)
    ang = jnp.arange(seq).astype(F32)[:, None] * inv[None, :]
    cos, sin = jnp.cos(ang), jnp.sin(ang)
    lane = np.arange(LANES) % DA_QK_DIM
    idx = lane % half
    is_lo = jnp.asarray(lane < half)[None, :]
    is_hi = jnp.asarray((lane >= half) & (lane < ROPE_DIM))[None, :]
    cos_l, sin_l = cos[:, idx], sin[:, idx]
    c = jnp.where(is_lo | is_hi, cos_l, 1.0)
    sa = jnp.where(is_lo, -sin_l, 0.0)
    sb = jnp.where(is_hi, sin_l, 0.0)
    return c, sa, sb


def _na_bias_tables(rpb, rows):
    r_step, win = NA_ROWS_PER_STEP, NA_WIN_ROWS
    reps = [(0, 0), (r_step, r_step - NA_KR // 2), (rows - r_step, rows - win)]
    rq = np.arange(r_step)[:, None, None, None]
    cq = np.arange(GRID_W)[None, :, None, None]
    ik = np.arange(win)[None, None, :, None]
    ck = np.arange(GRID_W)[None, None, None, :]
    c_start = np.clip(cq - NA_KC // 2, 0, GRID_W - NA_KC)
    col_ok = (ck >= c_start) & (ck < c_start + NA_KC)
    col_off = np.clip(ck - cq, -(NA_KC - 1), NA_KC - 1) + NA_KC - 1
    tabs = []
    for r0, ks in reps:
        r = r0 + rq
        rs = np.clip(r - NA_KR // 2, 0, rows - NA_KR)
        krow = ks + ik
        row_ok = (krow >= rs) & (krow < rs + NA_KR)
        row_off = np.clip(krow - r + NA_KR - 1, 0, 2 * NA_KR - 2)
        shape = (r_step, GRID_W, win, GRID_W)
        ridx = np.broadcast_to(row_off, shape).reshape(r_step * GRID_W, win * GRID_W)
        cidx = np.broadcast_to(col_off, shape).reshape(r_step * GRID_W, win * GRID_W)
        ok = np.broadcast_to(row_ok & col_ok, shape).reshape(r_step * GRID_W, win * GRID_W)
        vals = rpb[:, ridx, cidx].astype(F32)
        tabs.append(jnp.where(jnp.asarray(ok)[None], vals, -jnp.inf))
    return jnp.stack(tabs, axis=0)


def _na_kernel(q_ref, k_ref, v_ref, tab_ref, g_ref, o_ref, *, rows):
    step = pl.program_id(2)
    ks = jnp.clip(step * NA_ROWS_PER_STEP - NA_KR // 2, 0, rows - NA_WIN_ROWS)
    start = pl.multiple_of(ks * GRID_W, 256)
    nkeys = NA_WIN_ROWS * GRID_W
    k = k_ref[0, pl.ds(start, nkeys), :]
    v = v_ref[0, pl.ds(start, nkeys), :]
    scale = HEAD_DIM ** -0.5
    for j in range(NA_ROWS_PER_STEP * GRID_W // NA_SUB_Q):
        sl = slice(j * NA_SUB_Q, (j + 1) * NA_SUB_Q)
        s = lax.dot_general(q_ref[0, sl, :], k, (((1,), (1,)), ((), ())),
                            preferred_element_type=F32)
        s = s * scale + tab_ref[0, 0, sl, :]
        m = jnp.max(s, axis=-1, keepdims=True)
        e = jnp.exp(s - m)
        l = jnp.sum(e, axis=-1, keepdims=True)
        o = jnp.dot(e.astype(BF16), v, preferred_element_type=F32) / l
        o_ref[0, sl, :] = (o * g_ref[0, sl, :]).astype(o_ref.dtype)


def _neighbourhood_attention(qkv, gate, tabs, n_heads, q_blk, k_blk, v_blk, g_blk):
    b, s, _ = qkv.shape
    rows = s // GRID_W
    assert rows % NA_ROWS_PER_STEP == 0 and rows >= NA_WIN_ROWS
    n_steps = rows // NA_ROWS_PER_STEP
    tq = NA_ROWS_PER_STEP * GRID_W

    def cls(i):
        return jnp.where(i == 0, 0, jnp.where(i == n_steps - 1, 2, 1))

    return pl.pallas_call(
        functools.partial(_na_kernel, rows=rows),
        out_shape=jax.ShapeDtypeStruct((b, s, n_heads * HEAD_DIM), BF16),
        grid=(b, n_heads, n_steps),
        in_specs=[
            pl.BlockSpec((1, tq, HEAD_DIM), lambda bi, h, i: (bi, i, q_blk + h)),
            pl.BlockSpec((1, s, HEAD_DIM), lambda bi, h, i: (bi, 0, k_blk + h)),
            pl.BlockSpec((1, s, HEAD_DIM), lambda bi, h, i: (bi, 0, v_blk + h)),
            pl.BlockSpec((1, 1, tq, NA_WIN_ROWS * GRID_W), lambda bi, h, i: (cls(i), h, 0, 0)),
            pl.BlockSpec((1, tq, HEAD_DIM), lambda bi, h, i: (bi, i, g_blk + h)),
        ],
        out_specs=pl.BlockSpec((1, tq, HEAD_DIM), lambda bi, h, i: (bi, i, h)),
        compiler_params=_params("parallel", "parallel", "arbitrary"),
        name="neighbourhood_attention",
    )(qkv, qkv, qkv, tabs, gate)


def _da_kernel(lq1_ref, lk1_ref, lq2_ref, lk2_ref, sg_ref, q_ref, k_ref, v_ref, g_ref, o_ref,
               *, tk, lam_init):
    tq = q_ref.shape[1]
    seq = k_ref.shape[1]
    q = q_ref[0] * jnp.asarray(DA_QK_DIM ** -0.5, BF16)
    lane = lax.broadcasted_iota(jnp.int32, q.shape, 1)
    zero = jnp.zeros_like(q)
    q2 = jnp.concatenate([jnp.where(lane < DA_QK_DIM, q, zero),
                          jnp.where(lane >= DA_QK_DIM, q, zero)], axis=0)

    def body(c, carry):
        m, l, acc = carry
        off = pl.multiple_of(c * tk, tk)
        k = k_ref[0, pl.ds(off, tk), :]
        v = v_ref[0, pl.ds(off, tk), :]
        s = lax.dot_general(q2, k, (((1,), (1,)), ((), ())), preferred_element_type=F32)
        m_new = jnp.maximum(m, jnp.max(s, axis=-1, keepdims=True))
        alpha = jnp.exp(m - m_new)
        e = jnp.exp(s - m_new)
        l = alpha * l + jnp.sum(e, axis=-1, keepdims=True)
        acc = alpha * acc + jnp.dot(e.astype(BF16), v, preferred_element_type=F32)
        return m_new, l, acc

    init = (jnp.full((2 * tq, 1), -jnp.inf, F32), jnp.zeros((2 * tq, 1), F32),
            jnp.zeros((2 * tq, HEAD_DIM), F32))
    _, l, acc = lax.fori_loop(0, seq // tk, body, init)

    lam = (jnp.exp(jnp.sum(lq1_ref[...] * lk1_ref[...], axis=-1, keepdims=True))
           - jnp.exp(jnp.sum(lq2_ref[...] * lk2_ref[...], axis=-1, keepdims=True)) + lam_init)
    on = acc / l
    o = on[:tq] - lam * on[tq:]
    y = o * lax.rsqrt(jnp.mean(o * o, axis=-1, keepdims=True) + EPS)
    y = (y * sg_ref[...]) * (1.0 - lam_init)
    o_ref[0] = (y * g_ref[0]).astype(o_ref.dtype)


def _differential_attention(qk, v_arr, gate, lq1, lk1, lq2, lk2, subln_g, lam_init, n_heads,
                            q_blk, k_blk, v_blk, g_blk, tq=256, tk=512):
    b, s, _ = qk.shape
    tq, tk = min(tq, s), min(tk, s)
    small = lambda bi, h, i: (0, 0)
    vec = lambda a: a.reshape(1, -1).astype(F32)
    return pl.pallas_call(
        functools.partial(_da_kernel, tk=tk, lam_init=lam_init),
        out_shape=jax.ShapeDtypeStruct((b, s, n_heads * HEAD_DIM), BF16),
        grid=(b, n_heads, s // tq),
        in_specs=[
            pl.BlockSpec((1, DA_QK_DIM), small), pl.BlockSpec((1, DA_QK_DIM), small),
            pl.BlockSpec((1, DA_QK_DIM), small), pl.BlockSpec((1, DA_QK_DIM), small),
            pl.BlockSpec((1, HEAD_DIM), small),
            pl.BlockSpec((1, tq, HEAD_DIM), lambda bi, h, i: (bi, i, q_blk + h)),
            pl.BlockSpec((1, s, HEAD_DIM), lambda bi, h, i: (bi, 0, k_blk + h)),
            pl.BlockSpec((1, s, HEAD_DIM), lambda bi, h, i: (bi, 0, v_blk + h)),
            pl.BlockSpec((1, tq, HEAD_DIM), lambda bi, h, i: (bi, i, g_blk + h)),
        ],
        out_specs=pl.BlockSpec((1, tq, HEAD_DIM), lambda bi, h, i: (bi, i, h)),
        compiler_params=_params("parallel", "parallel", "arbitrary"),
        name="differential_attention",
    )(vec(lq1), vec(lk1), vec(lq2), vec(lk2), vec(subln_g), qk, qk, v_arr, gate)


def _merge_kernel(ua_ref, ub_ref, wa_ref, wb_ref, ga_ref, gb_ref, o_ref):
    pa = jnp.dot(ua_ref[...], wa_ref[...], preferred_element_type=F32)
    pb = jnp.dot(ub_ref[...], wb_ref[...], preferred_element_type=F32)
    o_ref[...] = (ga_ref[...] * pa + gb_ref[...] * pb).astype(o_ref.dtype)


def _merge(ua, ub, wa, wb, gates, tm=512, tn=1024):
    t, k = ua.shape
    n = wa.shape[1]
    tm, tn = min(tm, t), min(tn, n)
    nj = n // tn
    return pl.pallas_call(
        _merge_kernel,
        out_shape=jax.ShapeDtypeStruct((t, n), BF16),
        grid=(t // tm, nj),
        in_specs=[pl.BlockSpec((tm, k), lambda i, j: (i, 0)),
                  pl.BlockSpec((tm, k), lambda i, j: (i, 0)),
                  pl.BlockSpec((k, tn), lambda i, j: (0, j)),
                  pl.BlockSpec((k, tn), lambda i, j: (0, j)),
                  pl.BlockSpec((tm, tn), lambda i, j: (i, j)),
                  pl.BlockSpec((tm, tn), lambda i, j: (i, nj + j))],
        out_specs=pl.BlockSpec((tm, tn), lambda i, j: (i, j)),
        compiler_params=_params("parallel", "parallel"),
        name="gated_merge",
    )(ua, ub, wa, wb, gates, gates)


def _out_proj_kernel(m_ref, w_ref, x_ref, o_ref):
    o_ref[...] = x_ref[...] + jnp.dot(m_ref[...], w_ref[...], preferred_element_type=F32)


def _out_proj(merged, wo, x, tm=1024, tn=512):
    t, k = merged.shape
    n = wo.shape[1]
    tm, tn = min(tm, t), min(tn, n)
    return pl.pallas_call(
        _out_proj_kernel,
        out_shape=jax.ShapeDtypeStruct((t, n), F32),
        grid=(t // tm, n // tn),
        in_specs=[pl.BlockSpec((tm, k), lambda i, j: (i, 0)),
                  pl.BlockSpec((k, tn), lambda i, j: (0, j)),
                  pl.BlockSpec((tm, tn), lambda i, j: (i, j))],
        out_specs=pl.BlockSpec((tm, tn), lambda i, j: (i, j)),
        compiler_params=_params("parallel", "parallel"),
        name="out_proj_residual",
    )(merged, wo, x)


def _layer_weights(w_in_l, w_up_a_l, w_up_b_l, w_o_l, d_model):
    bw = d_model // 2
    col = lambda i: w_in_l[:, i * bw:(i + 1) * bw]
    return dict(
        plain=jnp.concatenate([col(0), col(1), col(2), col(6)], axis=1).astype(BF16),
        rope=jnp.concatenate([col(4), col(5)], axis=1).astype(BF16),
        gate=jnp.concatenate([col(3), col(7)], axis=1).astype(BF16),
        merge=w_in_l[:, 8 * bw:].astype(BF16),
        up_a=w_up_a_l.astype(BF16), up_b=w_up_b_l.astype(BF16), out=w_o_l.astype(BF16),
    )


def _layer(x, batch, seq, w, g_norm, b_merge, na_tabs, rope_tabs, lam_params, subln_g, lam_init):
    t, d = x.shape
    bw = d // 2
    n_heads = bw // HEAD_DIM
    h = _rmsnorm(x, g_norm, BF16)
    plain = _proj(_proj_plain_kernel, h, w["plain"], BF16, name="proj_plain")
    tm = min(1024, seq)
    n_pos_blocks = seq // tm
    tab_spec = pl.BlockSpec((tm, LANES), lambda i, j: (i % n_pos_blocks, 0))
    roped = _proj(_proj_rope_kernel, h, w["rope"], BF16, extra=rope_tabs,
                  extra_specs=[tab_spec] * 3, tm=tm, name="proj_rope")
    silu = _proj(_proj_silu_kernel, h, w["gate"], F32, name="proj_silu")
    tn = min(1024, 2 * d)
    gates = _proj(_proj_sigmoid_kernel, h, w["merge"], F32, extra=(b_merge.reshape(1, -1),),
                  extra_specs=[pl.BlockSpec((1, tn), lambda i, j: (0, j))], tn=tn,
                  name="proj_sigmoid")

    plain3 = plain.reshape(batch, seq, 4 * bw)
    silu3 = silu.reshape(batch, seq, 2 * bw)
    ua = _neighbourhood_attention(plain3, silu3, na_tabs, n_heads,
                                  q_blk=0, k_blk=n_heads, v_blk=2 * n_heads, g_blk=0)
    ub = _differential_attention(roped.reshape(batch, seq, 2 * bw), plain3, silu3, *lam_params,
                                 subln_g, lam_init, n_heads,
                                 q_blk=0, k_blk=n_heads, v_blk=3 * n_heads, g_blk=n_heads)
    merged = _merge(ua.reshape(t, bw), ub.reshape(t, bw), w["up_a"], w["up_b"], gates)
    return _out_proj(merged, w["out"], x)


def kernel(x_prompt, x_sample, norm_g, w_in, b_merge, rpb, lambda_q1, lambda_k1, lambda_q2,
           lambda_k2, subln_g, w_up_a, w_up_b, w_o, final_g):
    depth = norm_g.shape[0]
    d_model = x_prompt.shape[-1]
    weights = [_layer_weights(w_in[l], w_up_a[l], w_up_b[l], w_o[l], d_model) for l in range(depth)]
    outs = []
    for x in (x_prompt, x_sample):
        batch, seq, _ = x.shape
        rope_tabs = _rope_tables(seq)
        xt = x.reshape(batch * seq, d_model)
        for l in range(depth):
            lam_init = 0.8 - 0.6 * math.exp(-0.3 * l)
            na_tabs = _na_bias_tables(rpb[l], seq // GRID_W)
            lam_params = (lambda_q1[l], lambda_k1[l], lambda_q2[l], lambda_k2[l])
            xt = _layer(xt, batch, seq, weights[l], norm_g[l], b_merge[l], na_tabs, rope_tabs,
                        lam_params, subln_g[l], lam_init)
        outs.append(_rmsnorm(xt, final_g, F32).reshape(batch, seq, d_model))
    return tuple(outs)
```

```python
import functools
import math

import numpy as np
import jax
import jax.numpy as jnp
from jax import lax
from jax.experimental import pallas as pl
from jax.experimental.pallas import tpu as pltpu

GRID_W = 64
HEAD_DIM = 128
NA_KR = 8
NA_KC = 16
DA_QK_DIM = 64
ROPE_THETA = 500000.0
ROPE_DIM = 16
EPS = 1e-6
LOG2E = math.log2(math.e)

VMEM_LIMIT_BYTES = 56 * 1024 * 1024
LANES = 128

NA_ROWS_PER_STEP = 8
NA_SUB_ROWS = 2
NA_WIN_ROWS = NA_SUB_ROWS + NA_KR
NA_CLASSES = 5

F32 = jnp.float32
BF16 = jnp.bfloat16
_NT = (((1,), (1,)), ((), ()))


def _params(*sem):
    return pltpu.CompilerParams(dimension_semantics=sem, vmem_limit_bytes=VMEM_LIMIT_BYTES)


def _rmsnorm_kernel(x_ref, g_ref, o_ref):
    x = x_ref[...]
    y = x * lax.rsqrt(jnp.mean(x * x, axis=-1, keepdims=True) + EPS)
    o_ref[...] = (y * g_ref[...]).astype(o_ref.dtype)


def _rmsnorm(x, g, out_dtype):
    t, d = x.shape
    tm = min(256, t)
    return pl.pallas_call(
        _rmsnorm_kernel,
        out_shape=jax.ShapeDtypeStruct((t, d), out_dtype),
        grid=(t // tm,),
        in_specs=[pl.BlockSpec((tm, d), lambda i: (i, 0)),
                  pl.BlockSpec((1, d), lambda i: (0, 0))],
        out_specs=pl.BlockSpec((tm, d), lambda i: (i, 0)),
        compiler_params=_params("parallel"),
        name="rmsnorm",
    )(x, g.reshape(1, d))


def _proj_scaled_kernel(h_ref, w_ref, scale_ref, o_ref):
    acc = jnp.dot(h_ref[...], w_ref[...], preferred_element_type=F32)
    o_ref[...] = (acc * scale_ref[...]).astype(o_ref.dtype)


def _proj_silu_kernel(h_ref, w_ref, o_ref):
    acc = jnp.dot(h_ref[...], w_ref[...], preferred_element_type=F32)
    o_ref[...] = (acc * jax.nn.sigmoid(acc)).astype(o_ref.dtype)


def _proj_sigmoid_kernel(h_ref, w_ref, b_ref, o_ref):
    acc = jnp.dot(h_ref[...], w_ref[...], preferred_element_type=F32)
    o_ref[...] = jax.nn.sigmoid(acc + b_ref[...]).astype(o_ref.dtype)


def _proj_rope_kernel(h_ref, w_ref, scale_ref, c_ref, sa_ref, sb_ref, o_ref):
    acc = jnp.dot(h_ref[...], w_ref[...], preferred_element_type=F32)
    c, sa, sb = c_ref[...], sa_ref[...], sb_ref[...]
    half = ROPE_DIM // 2
    for j in range(acc.shape[1] // LANES):
        sl = slice(j * LANES, (j + 1) * LANES)
        z = acc[:, sl]
        up = pltpu.roll(z, LANES - half, 1)
        dn = pltpu.roll(z, half, 1)
        o_ref[:, sl] = ((z * c + up * sa + dn * sb) * scale_ref[:, sl]).astype(o_ref.dtype)


def _proj(kernel_fn, h, w, out_dtype, extra=(), extra_specs=(), tm=1024, tn=1024, name="proj"):
    t, k = h.shape
    n = w.shape[1]
    tm, tn = min(tm, t), min(tn, n)
    return pl.pallas_call(
        kernel_fn,
        out_shape=jax.ShapeDtypeStruct((t, n), out_dtype),
        grid=(t // tm, n // tn),
        in_specs=[pl.BlockSpec((tm, k), lambda i, j: (i, 0)),
                  pl.BlockSpec((k, tn), lambda i, j: (0, j))] + list(extra_specs),
        out_specs=pl.BlockSpec((tm, tn), lambda i, j: (i, j)),
        compiler_params=_params("parallel", "parallel"),
        name=name,
    )(h, w, *extra)


def _row_spec(tn):
    return pl.BlockSpec((1, tn), lambda i, j: (0, j))


def _rope_tables(seq):
    half = ROPE_DIM // 2
    inv = ROPE_THETA ** (-jnp.arange(0, ROPE_DIM, 2, dtype=F32) / ROPE_DIM)
    ang = jnp.arange(seq).astype(F32)[:, None] * inv[None, :]
    cos, sin = jnp.cos(ang), jnp.sin(ang)
    lane = np.arange(LANES) % DA_QK_DIM
    idx = lane % half
    is_lo = jnp.asarray(lane < half)[None, :]
    is_hi = jnp.asarray((lane >= half) & (lane < ROPE_DIM))[None, :]
    cos_l, sin_l = cos[:, idx], sin[:, idx]
    c = jnp.where(is_lo | is_hi, cos_l, 1.0)
    sa = jnp.where(is_lo, -sin_l, 0.0)
    sb = jnp.where(is_hi, sin_l, 0.0)
    return c, sa, sb


def _na_class_rows(rows):
    return [0, NA_SUB_ROWS, 2 * NA_SUB_ROWS, rows - 2 * NA_SUB_ROWS, rows - NA_SUB_ROWS]


def _na_window_start(r, rows):
    return np.clip(r - NA_KR // 2, 0, rows - NA_WIN_ROWS)


def _na_bias_tables(rpb, rows):
    n_heads = rpb.shape[0]
    pad = GRID_W - NA_KC
    u = jnp.pad(rpb.astype(F32) * LOG2E, ((0, 0), (0, 0), (pad, pad)), mode="edge")
    tiles = jnp.stack([u[:, :, GRID_W - 1 - cq: 2 * GRID_W - 1 - cq] for cq in range(GRID_W)], axis=2)
    cq = np.arange(GRID_W)[:, None]
    ck = np.arange(GRID_W)[None, :]
    c_start = np.clip(cq - NA_KC // 2, 0, GRID_W - NA_KC)
    col_ok = jnp.asarray((ck >= c_start) & (ck < c_start + NA_KC))
    tiles = jnp.where(col_ok[None, None], tiles, -jnp.inf)
    neg = jnp.full((n_heads, GRID_W, GRID_W), -jnp.inf, F32)
    tabs = []
    for r0 in _na_class_rows(rows):
        ws = int(_na_window_start(r0, rows))
        row_blocks = []
        for rq in range(NA_SUB_ROWS):
            r = r0 + rq
            rs = int(np.clip(r - NA_KR // 2, 0, rows - NA_KR))
            pieces = []
            for ik in range(NA_WIN_ROWS):
                krow = ws + ik
                ok = rs <= krow < rs + NA_KR
                pieces.append(tiles[:, krow - r + NA_KR - 1] if ok else neg)
            row_blocks.append(jnp.concatenate(pieces, axis=-1))
        tabs.append(jnp.concatenate(row_blocks, axis=1))
    return jnp.stack(tabs, axis=0)


def _na_kernel(q_ref, k_ref, v_ref, tab_ref, g_ref, o_ref, *, rows):
    step = pl.program_id(2)
    sub_q = NA_SUB_ROWS * GRID_W
    nkeys = NA_WIN_ROWS * GRID_W
    for j in range(NA_ROWS_PER_STEP // NA_SUB_ROWS):
        r = step * NA_ROWS_PER_STEP + j * NA_SUB_ROWS
        ws = jnp.clip(r - NA_KR // 2, 0, rows - NA_WIN_ROWS)
        start = pl.multiple_of(ws * GRID_W, NA_SUB_ROWS * GRID_W)
        cls = jnp.where(r == 0, 0, jnp.where(r == NA_SUB_ROWS, 1, jnp.where(
            r == rows - 2 * NA_SUB_ROWS, 3, jnp.where(r == rows - NA_SUB_ROWS, 4, 2))))
        sl = slice(j * sub_q, (j + 1) * sub_q)
        k = k_ref[0, pl.ds(start, nkeys), :]
        v = v_ref[0, pl.ds(start, nkeys), :]
        s = lax.dot_general(q_ref[0, sl, :], k, _NT, preferred_element_type=F32) + tab_ref[cls, 0]
        m = jnp.max(s, axis=-1, keepdims=True)
        e = jnp.exp2(s - m)
        l = jnp.sum(e, axis=-1, keepdims=True)
        o = jnp.dot(e.astype(BF16), v, preferred_element_type=F32) / l
        o_ref[0, sl, :] = (o * g_ref[0, sl, :]).astype(o_ref.dtype)


def _neighbourhood_attention(qkv, gate, tabs, n_heads, q_blk, k_blk, v_blk, g_blk):
    b, s, _ = qkv.shape
    rows = s // GRID_W
    assert rows % NA_ROWS_PER_STEP == 0 and rows >= 2 * NA_ROWS_PER_STEP
    tq = NA_ROWS_PER_STEP * GRID_W
    nkeys = NA_WIN_ROWS * GRID_W
    return pl.pallas_call(
        functools.partial(_na_kernel, rows=rows),
        out_shape=jax.ShapeDtypeStruct((b, s, n_heads * HEAD_DIM), BF16),
        grid=(n_heads, b, rows // NA_ROWS_PER_STEP),
        in_specs=[
            pl.BlockSpec((1, tq, HEAD_DIM), lambda h, bi, i: (bi, i, q_blk + h)),
            pl.BlockSpec((1, s, HEAD_DIM), lambda h, bi, i: (bi, 0, k_blk + h)),
            pl.BlockSpec((1, s, HEAD_DIM), lambda h, bi, i: (bi, 0, v_blk + h)),
            pl.BlockSpec((NA_CLASSES, 1, NA_SUB_ROWS * GRID_W, nkeys), lambda h, bi, i: (0, h, 0, 0)),
            pl.BlockSpec((1, tq, HEAD_DIM), lambda h, bi, i: (bi, i, g_blk + h)),
        ],
        out_specs=pl.BlockSpec((1, tq, HEAD_DIM), lambda h, bi, i: (bi, i, h)),
        compiler_params=_params("parallel", "parallel", "arbitrary"),
        name="neighbourhood_attention",
    )(qkv, qkv, qkv, tabs, gate)


DA_SLAB = 16


def _da_kernel(lq1_ref, lk1_ref, lq2_ref, lk2_ref, sg_ref, q_ref, k_ref, v_ref, g_ref, o_ref,
               vt_ref, s0_ref, s1_ref, e0_ref, e1_ref, acc_ref, *, tq, tk, lam_init):
    seq = k_ref.shape[1]
    n_chunks = seq // tk

    for c in range(n_chunks):
        vt_ref[c] = v_ref[0, c * tk:(c + 1) * tk, :].astype(F32).T.astype(BF16)

    lam = (jnp.exp(jnp.sum(lq1_ref[...] * lk1_ref[...], axis=-1, keepdims=True))
           - jnp.exp(jnp.sum(lq2_ref[...] * lk2_ref[...], axis=-1, keepdims=True)) + lam_init)
    sg = sg_ref[...]

    def q_block(qb, _):
        qoff = pl.multiple_of(qb * tq, tq)
        q = q_ref[0, pl.ds(qoff, tq), :]
        lane = lax.broadcasted_iota(jnp.int32, q.shape, 1)
        zero = jnp.zeros_like(q)
        q2 = jnp.concatenate([jnp.where(lane < DA_QK_DIM, q, zero),
                              jnp.where(lane >= DA_QK_DIM, q, zero)], axis=0)

        def scores(c, s_ref):
            koff = pl.multiple_of(c * tk, tk)
            s = lax.dot_general(k_ref[0, pl.ds(koff, tk), :], q2, _NT, preferred_element_type=F32)
            s_ref[...] = s
            part = s[:DA_SLAB]
            for r in range(DA_SLAB, tk, DA_SLAB):
                part = jnp.maximum(part, s[r:r + DA_SLAB])
            return jnp.max(part, axis=0, keepdims=True)

        def softmax(s_ref, e_ref, cmax, m, l):
            m_new = jnp.maximum(m, cmax)
            alpha = jnp.exp2(m - m_new)
            part = None
            for r in range(0, tk, DA_SLAB):
                e = jnp.exp2(s_ref[r:r + DA_SLAB, :] - m_new)
                e_ref[r:r + DA_SLAB, :] = e.astype(BF16)
                part = e if part is None else part + e
            l = alpha * l + jnp.sum(part, axis=0, keepdims=True)
            return m_new, l, alpha

        def accumulate(c, e_ref, alpha):
            acc_ref[...] = alpha * acc_ref[...] + jnp.dot(vt_ref[c], e_ref[...],
                                                          preferred_element_type=F32)

        acc_ref[...] = jnp.zeros_like(acc_ref)
        m = jnp.full((1, 2 * tq), -jnp.inf, F32)
        l = jnp.zeros((1, 2 * tq), F32)
        cmax = scores(0, s0_ref)
        cmax_next = scores(1, s1_ref)
        m, l, alpha = softmax(s0_ref, e0_ref, cmax, m, l)
        cmax = cmax_next

        def pair(p, carry):
            m, l, cmax, alpha = carry
            i = 2 * p + 2
            m, l, alpha_next = softmax(s1_ref, e1_ref, cmax, m, l)
            accumulate(i - 2, e0_ref, alpha)
            cmax, alpha = scores(i, s0_ref), alpha_next
            m, l, alpha_next = softmax(s0_ref, e0_ref, cmax, m, l)
            accumulate(i - 1, e1_ref, alpha)
            return m, l, scores(i + 1, s1_ref), alpha_next

        m, l, cmax, alpha = lax.fori_loop(0, (n_chunks - 2) // 2, pair, (m, l, cmax, alpha))
        m, l, alpha_last = softmax(s1_ref, e1_ref, cmax, m, l)
        accumulate(n_chunks - 2, e0_ref, alpha)
        accumulate(n_chunks - 1, e1_ref, alpha_last)

        on = acc_ref[...] / l
        o = on[:, :tq] - lam * on[:, tq:]
        y = o * lax.rsqrt(jnp.mean(o * o, axis=0, keepdims=True) + EPS)
        y = (y * sg) * (1.0 - lam_init)
        o_ref[0, pl.ds(qoff, tq), :] = (y.T * g_ref[0, pl.ds(qoff, tq), :]).astype(o_ref.dtype)
        return 0

    lax.fori_loop(0, seq // tq, q_block, 0)


def _differential_attention(qk, v_arr, gate, lq1, lk1, lq2, lk2, subln_g, lam_init, n_heads,
                            q_blk, k_blk, v_blk, g_blk, tq=256, tk=512):
    b, s, _ = qk.shape
    tq, tk = min(tq, s), min(tk, s // 2)
    assert (s // tk) % 2 == 0 and s % tq == 0
    small = lambda bi, h: (0, 0)
    vec = lambda a: a.reshape(1, -1).astype(F32)
    head = lambda blk: pl.BlockSpec((1, s, HEAD_DIM), lambda bi, h: (bi, 0, blk + h))
    return pl.pallas_call(
        functools.partial(_da_kernel, tq=tq, tk=tk, lam_init=lam_init),
        out_shape=jax.ShapeDtypeStruct((b, s, n_heads * HEAD_DIM), BF16),
        grid=(b, n_heads),
        in_specs=[
            pl.BlockSpec((1, DA_QK_DIM), small), pl.BlockSpec((1, DA_QK_DIM), small),
            pl.BlockSpec((1, DA_QK_DIM), small), pl.BlockSpec((1, DA_QK_DIM), small),
            pl.BlockSpec((HEAD_DIM, 1), small),
            head(q_blk), head(k_blk), head(v_blk), head(g_blk),
        ],
        out_specs=pl.BlockSpec((1, s, HEAD_DIM), lambda bi, h: (bi, 0, h)),
        scratch_shapes=[pltpu.VMEM((s // tk, HEAD_DIM, tk), BF16),
                        pltpu.VMEM((tk, 2 * tq), F32),
                        pltpu.VMEM((tk, 2 * tq), F32),
                        pltpu.VMEM((tk, 2 * tq), BF16),
                        pltpu.VMEM((tk, 2 * tq), BF16),
                        pltpu.VMEM((HEAD_DIM, 2 * tq), F32)],
        compiler_params=_params("parallel", "parallel"),
        name="differential_attention",
    )(vec(lq1), vec(lk1), vec(lq2), vec(lk2), subln_g.reshape(-1, 1).astype(F32),
      qk, qk, v_arr, gate)


def _merge_kernel(ua_ref, ub_ref, wa_ref, wb_ref, ga_ref, gb_ref, o_ref):
    pa = jnp.dot(ua_ref[...], wa_ref[...], preferred_element_type=F32)
    pb = jnp.dot(ub_ref[...], wb_ref[...], preferred_element_type=F32)
    o_ref[...] = (ga_ref[...] * pa + gb_ref[...] * pb).astype(o_ref.dtype)


def _merge(ua, ub, wa, wb, gates, tm=512, tn=1024):
    t, k = ua.shape
    n = wa.shape[1]
    tm, tn = min(tm, t), min(tn, n)
    nj = n // tn
    return pl.pallas_call(
        _merge_kernel,
        out_shape=jax.ShapeDtypeStruct((t, n), BF16),
        grid=(t // tm, nj),
        in_specs=[pl.BlockSpec((tm, k), lambda i, j: (i, 0)),
                  pl.BlockSpec((tm, k), lambda i, j: (i, 0)),
                  pl.BlockSpec((k, tn), lambda i, j: (0, j)),
                  pl.BlockSpec((k, tn), lambda i, j: (0, j)),
                  pl.BlockSpec((tm, tn), lambda i, j: (i, j)),
                  pl.BlockSpec((tm, tn), lambda i, j: (i, nj + j))],
        out_specs=pl.BlockSpec((tm, tn), lambda i, j: (i, j)),
        compiler_params=_params("parallel", "parallel"),
        name="gated_merge",
    )(ua, ub, wa, wb, gates, gates)


def _out_proj_kernel(m_ref, w_ref, x_ref, o_ref):
    o_ref[...] = x_ref[...] + jnp.dot(m_ref[...], w_ref[...], preferred_element_type=F32)


def _out_proj(merged, wo, x, tm=1024, tn=512):
    t, k = merged.shape
    n = wo.shape[1]
    tm, tn = min(tm, t), min(tn, n)
    return pl.pallas_call(
        _out_proj_kernel,
        out_shape=jax.ShapeDtypeStruct((t, n), F32),
        grid=(t // tm, n // tn),
        in_specs=[pl.BlockSpec((tm, k), lambda i, j: (i, 0)),
                  pl.BlockSpec((k, tn), lambda i, j: (0, j)),
                  pl.BlockSpec((tm, tn), lambda i, j: (i, j))],
        out_specs=pl.BlockSpec((tm, tn), lambda i, j: (i, j)),
        compiler_params=_params("parallel", "parallel"),
        name="out_proj_residual",
    )(merged, wo, x)


def _layer_weights(w_in_l, w_up_a_l, w_up_b_l, w_o_l, d_model):
    bw = d_model // 2
    col = lambda i: w_in_l[:, i * bw:(i + 1) * bw]
    ones = jnp.ones((1, bw), F32)
    return dict(
        plain=jnp.concatenate([col(0), col(1), col(2), col(6)], axis=1).astype(BF16),
        plain_scale=jnp.concatenate([ones * (HEAD_DIM ** -0.5 * LOG2E), ones, ones, ones], axis=1),
        rope=jnp.concatenate([col(4), col(5)], axis=1).astype(BF16),
        rope_scale=jnp.concatenate([ones * (DA_QK_DIM ** -0.5 * LOG2E), ones], axis=1),
        gate=jnp.concatenate([col(3), col(7)], axis=1).astype(BF16),
        merge=w_in_l[:, 8 * bw:].astype(BF16),
        up_a=w_up_a_l.astype(BF16), up_b=w_up_b_l.astype(BF16), out=w_o_l.astype(BF16),
    )


def _layer(x, batch, seq, w, g_norm, b_merge, na_tabs, rope_tabs, lam_params, subln_g, lam_init):
    t, d = x.shape
    bw = d // 2
    n_heads = bw // HEAD_DIM
    h = _rmsnorm(x, g_norm, BF16)
    tn = min(1024, bw)
    plain = _proj(_proj_scaled_kernel, h, w["plain"], BF16, extra=(w["plain_scale"],),
                  extra_specs=[_row_spec(tn)], tn=tn, name="proj_plain")
    tm = min(1024, seq)
    n_pos_blocks = seq // tm
    tab_spec = pl.BlockSpec((tm, LANES), lambda i, j: (i % n_pos_blocks, 0))
    roped = _proj(_proj_rope_kernel, h, w["rope"], BF16, extra=(w["rope_scale"],) + tuple(rope_tabs),
                  extra_specs=[_row_spec(tn)] + [tab_spec] * 3, tm=tm, tn=tn, name="proj_rope")
    silu = _proj(_proj_silu_kernel, h, w["gate"], F32, tn=tn, name="proj_silu")
    gates = _proj(_proj_sigmoid_kernel, h, w["merge"], F32, extra=(b_merge.reshape(1, -1),),
                  extra_specs=[_row_spec(tn)], tn=tn, name="proj_sigmoid")

    plain3 = plain.reshape(batch, seq, 4 * bw)
    silu3 = silu.reshape(batch, seq, 2 * bw)
    ua = _neighbourhood_attention(plain3, silu3, na_tabs, n_heads,
                                  q_blk=0, k_blk=n_heads, v_blk=2 * n_heads, g_blk=0)
    ub = _differential_attention(roped.reshape(batch, seq, 2 * bw), plain3, silu3, *lam_params,
                                 subln_g, lam_init, n_heads,
                                 q_blk=0, k_blk=n_heads, v_blk=3 * n_heads, g_blk=n_heads)
    merged = _merge(ua.reshape(t, bw), ub.reshape(t, bw), w["up_a"], w["up_b"], gates)
    return _out_proj(merged, w["out"], x)


def kernel(x_prompt, x_sample, norm_g, w_in, b_merge, rpb, lambda_q1, lambda_k1, lambda_q2,
           lambda_k2, subln_g, w_up_a, w_up_b, w_o, final_g):
    depth = norm_g.shape[0]
    d_model = x_prompt.shape[-1]
    weights = [_layer_weights(w_in[l], w_up_a[l], w_up_b[l], w_o[l], d_model) for l in range(depth)]
    outs = []
    for x in (x_prompt, x_sample):
        batch, seq, _ = x.shape
        rope_tabs = _rope_tables(seq)
        xt = x.reshape(batch * seq, d_model)
        for l in range(depth):
            lam_init = 0.8 - 0.6 * math.exp(-0.3 * l)
            na_tabs = _na_bias_tables(rpb[l], seq // GRID_W)
            lam_params = (lambda_q1[l], lambda_k1[l], lambda_q2[l], lambda_k2[l])
            xt = _layer(xt, batch, seq, weights[l], norm_g[l], b_merge[l], na_tabs, rope_tabs,
                        lam_params, subln_g[l], lam_init)
        outs.append(_rmsnorm(xt, final_g, F32).reshape(batch, seq, d_model))
    return tuple(outs)
```

```python
import functools
import math

import numpy as np
import jax
import jax.numpy as jnp
from jax import lax
from jax.experimental import pallas as pl
from jax.experimental.pallas import tpu as pltpu

GRID_W = 64
HEAD_DIM = 128
NA_KR = 8
NA_KC = 16
DA_QK_DIM = 64
ROPE_THETA = 500000.0
ROPE_DIM = 16
EPS = 1e-6
LOG2E = math.log2(math.e)

VMEM_LIMIT_BYTES = 56 * 1024 * 1024
LANES = 128

NA_ROWS_PER_STEP = 8
NA_SUB_ROWS = 2
NA_WIN_ROWS = NA_SUB_ROWS + NA_KR
NA_CLASSES = 5

F32 = jnp.float32
BF16 = jnp.bfloat16
_NT = (((1,), (1,)), ((), ()))


def _params(*sem):
    return pltpu.CompilerParams(dimension_semantics=sem, vmem_limit_bytes=VMEM_LIMIT_BYTES)


def _rmsnorm_kernel(x_ref, g_ref, o_ref):
    x = x_ref[...]
    y = x * lax.rsqrt(jnp.mean(x * x, axis=-1, keepdims=True) + EPS)
    o_ref[...] = (y * g_ref[...]).astype(o_ref.dtype)


def _rmsnorm(x, g, out_dtype):
    t, d = x.shape
    tm = min(256, t)
    return pl.pallas_call(
        _rmsnorm_kernel,
        out_shape=jax.ShapeDtypeStruct((t, d), out_dtype),
        grid=(t // tm,),
        in_specs=[pl.BlockSpec((tm, d), lambda i: (i, 0)),
                  pl.BlockSpec((1, d), lambda i: (0, 0))],
        out_specs=pl.BlockSpec((tm, d), lambda i: (i, 0)),
        compiler_params=_params("parallel"),
        name="rmsnorm",
    )(x, g.reshape(1, d))


def _proj_scaled_kernel(h_ref, w_ref, scale_ref, o_ref):
    acc = jnp.dot(h_ref[...], w_ref[...], preferred_element_type=F32)
    o_ref[...] = (acc * scale_ref[...]).astype(o_ref.dtype)


def _proj_silu_kernel(h_ref, w_ref, o_ref):
    acc = jnp.dot(h_ref[...], w_ref[...], preferred_element_type=F32)
    o_ref[...] = (acc * jax.nn.sigmoid(acc)).astype(o_ref.dtype)


def _proj_sigmoid_kernel(h_ref, w_ref, b_ref, o_ref):
    acc = jnp.dot(h_ref[...], w_ref[...], preferred_element_type=F32)
    o_ref[...] = jax.nn.sigmoid(acc + b_ref[...]).astype(o_ref.dtype)


def _proj_rope_kernel(h_ref, w_ref, scale_ref, c_ref, sa_ref, sb_ref, o_ref):
    acc = jnp.dot(h_ref[...], w_ref[...], preferred_element_type=F32)
    c, sa, sb = c_ref[...], sa_ref[...], sb_ref[...]
    half = ROPE_DIM // 2
    for j in range(acc.shape[1] // LANES):
        sl = slice(j * LANES, (j + 1) * LANES)
        z = acc[:, sl]
        up = pltpu.roll(z, LANES - half, 1)
        dn = pltpu.roll(z, half, 1)
        o_ref[:, sl] = ((z * c + up * sa + dn * sb) * scale_ref[:, sl]).astype(o_ref.dtype)


def _proj(kernel_fn, h, w, out_dtype, extra=(), extra_specs=(), tm=1024, tn=1024, name="proj"):
    t, k = h.shape
    n = w.shape[1]
    tm, tn = min(tm, t), min(tn, n)
    return pl.pallas_call(
        kernel_fn,
        out_shape=jax.ShapeDtypeStruct((t, n), out_dtype),
        grid=(t // tm, n // tn),
        in_specs=[pl.BlockSpec((tm, k), lambda i, j: (i, 0)),
                  pl.BlockSpec((k, tn), lambda i, j: (0, j))] + list(extra_specs),
        out_specs=pl.BlockSpec((tm, tn), lambda i, j: (i, j)),
        compiler_params=_params("parallel", "parallel"),
        name=name,
    )(h, w, *extra)


def _row_spec(tn):
    return pl.BlockSpec((1, tn), lambda i, j: (0, j))


def _rope_tables(seq):
    half = ROPE_DIM // 2
    inv = ROPE_THETA ** (-jnp.arange(0, ROPE_DIM, 2, dtype=F32) / ROPE_DIM)
    ang = jnp.arange(seq).astype(F32)[:, None] * inv[None, :]
    cos, sin = jnp.cos(ang), jnp.sin(ang)
    lane = np.arange(LANES) % DA_QK_DIM
    idx = lane % half
    is_lo = jnp.asarray(lane < half)[None, :]
    is_hi = jnp.asarray((lane >= half) & (lane < ROPE_DIM))[None, :]
    cos_l, sin_l = cos[:, idx], sin[:, idx]
    c = jnp.where(is_lo | is_hi, cos_l, 1.0)
    sa = jnp.where(is_lo, -sin_l, 0.0)
    sb = jnp.where(is_hi, sin_l, 0.0)
    return c, sa, sb


def _na_class_rows(rows):
    return [0, NA_SUB_ROWS, 2 * NA_SUB_ROWS, rows - 2 * NA_SUB_ROWS, rows - NA_SUB_ROWS]


def _na_window_start(r, rows):
    return np.clip(r - NA_KR // 2, 0, rows - NA_WIN_ROWS)


def _na_bias_tables(rpb, rows):
    n_heads = rpb.shape[0]
    pad = GRID_W - NA_KC
    u = jnp.pad(rpb.astype(F32) * LOG2E, ((0, 0), (0, 0), (pad, pad)), mode="edge")
    tiles = jnp.stack([u[:, :, GRID_W - 1 - cq: 2 * GRID_W - 1 - cq] for cq in range(GRID_W)], axis=2)
    cq = np.arange(GRID_W)[:, None]
    ck = np.arange(GRID_W)[None, :]
    c_start = np.clip(cq - NA_KC // 2, 0, GRID_W - NA_KC)
    col_ok = jnp.asarray((ck >= c_start) & (ck < c_start + NA_KC))
    tiles = jnp.where(col_ok[None, None], tiles, -jnp.inf)
    neg = jnp.full((n_heads, GRID_W, GRID_W), -jnp.inf, F32)
    tabs = []
    for r0 in _na_class_rows(rows):
        ws = int(_na_window_start(r0, rows))
        row_blocks = []
        for rq in range(NA_SUB_ROWS):
            r = r0 + rq
            rs = int(np.clip(r - NA_KR // 2, 0, rows - NA_KR))
            pieces = []
            for ik in range(NA_WIN_ROWS):
                krow = ws + ik
                ok = rs <= krow < rs + NA_KR
                pieces.append(tiles[:, krow - r + NA_KR - 1] if ok else neg)
            row_blocks.append(jnp.concatenate(pieces, axis=-1))
        tabs.append(jnp.concatenate(row_blocks, axis=1))
    return jnp.stack(tabs, axis=0)


def _na_kernel(q_ref, k_ref, v_ref, tab_ref, g_ref, o_ref, *, rows):
    step = pl.program_id(2)
    sub_q = NA_SUB_ROWS * GRID_W
    nkeys = NA_WIN_ROWS * GRID_W
    n_sub = NA_ROWS_PER_STEP // NA_SUB_ROWS
    starts, scores = [], []
    for j in range(n_sub):
        r = step * NA_ROWS_PER_STEP + j * NA_SUB_ROWS
        ws = jnp.clip(r - NA_KR // 2, 0, rows - NA_WIN_ROWS)
        start = pl.multiple_of(ws * GRID_W, NA_SUB_ROWS * GRID_W)
        cls = jnp.where(r == 0, 0, jnp.where(r == NA_SUB_ROWS, 1, jnp.where(
            r == rows - 2 * NA_SUB_ROWS, 3, jnp.where(r == rows - NA_SUB_ROWS, 4, 2))))
        k = k_ref[0, pl.ds(start, nkeys), :]
        s = lax.dot_general(q_ref[0, j * sub_q:(j + 1) * sub_q, :], k, _NT,
                            preferred_element_type=F32) + tab_ref[cls, 0]
        starts.append(start)
        scores.append(s)
    probs, sums = [], []
    for s in scores:
        e = jnp.exp2(s - jnp.max(s, axis=-1, keepdims=True))
        sums.append(jnp.sum(e, axis=-1, keepdims=True))
        probs.append(e.astype(BF16))
    for j in range(n_sub):
        sl = slice(j * sub_q, (j + 1) * sub_q)
        v = v_ref[0, pl.ds(starts[j], nkeys), :]
        o = jnp.dot(probs[j], v, preferred_element_type=F32) / sums[j]
        o_ref[0, sl, :] = (o * g_ref[0, sl, :]).astype(o_ref.dtype)


def _neighbourhood_attention(qkv, gate, tabs, n_heads, q_blk, k_blk, v_blk, g_blk):
    b, s, _ = qkv.shape
    rows = s // GRID_W
    assert rows % NA_ROWS_PER_STEP == 0 and rows >= 2 * NA_ROWS_PER_STEP
    tq = NA_ROWS_PER_STEP * GRID_W
    nkeys = NA_WIN_ROWS * GRID_W
    return pl.pallas_call(
        functools.partial(_na_kernel, rows=rows),
        out_shape=jax.ShapeDtypeStruct((b, s, n_heads * HEAD_DIM), BF16),
        grid=(n_heads, b, rows // NA_ROWS_PER_STEP),
        in_specs=[
            pl.BlockSpec((1, tq, HEAD_DIM), lambda h, bi, i: (bi, i, q_blk + h)),
            pl.BlockSpec((1, s, HEAD_DIM), lambda h, bi, i: (bi, 0, k_blk + h)),
            pl.BlockSpec((1, s, HEAD_DIM), lambda h, bi, i: (bi, 0, v_blk + h)),
            pl.BlockSpec((NA_CLASSES, 1, NA_SUB_ROWS * GRID_W, nkeys), lambda h, bi, i: (0, h, 0, 0)),
            pl.BlockSpec((1, tq, HEAD_DIM), lambda h, bi, i: (bi, i, g_blk + h)),
        ],
        out_specs=pl.BlockSpec((1, tq, HEAD_DIM), lambda h, bi, i: (bi, i, h)),
        compiler_params=_params("parallel", "parallel", "arbitrary"),
        name="neighbourhood_attention",
    )(qkv, qkv, qkv, tabs, gate)


DA_SLAB = 16
DA_ONES_ROWS = 16
DA_STREAMS = 2


def _da_kernel(lq1_ref, lk1_ref, lq2_ref, lk2_ref, sg_ref, q_ref, k_ref, v_ref, g_ref, o_ref,
               vt_ref, *stream_refs, tq, tk, lam_init):
    seq = k_ref.shape[1]
    n_chunks = seq // tk
    streams = [stream_refs[5 * u:5 * u + 5] for u in range(DA_STREAMS)]

    for c in range(n_chunks):
        vt_ref[c, :HEAD_DIM, :] = v_ref[0, c * tk:(c + 1) * tk, :].astype(F32).T.astype(BF16)
        vt_ref[c, HEAD_DIM:, :] = jnp.ones((DA_ONES_ROWS, tk), BF16)

    lam = (jnp.exp(jnp.sum(lq1_ref[...] * lk1_ref[...], axis=-1, keepdims=True))
           - jnp.exp(jnp.sum(lq2_ref[...] * lk2_ref[...], axis=-1, keepdims=True)) + lam_init)
    sg = sg_ref[...]

    def scores(q2, c, s_ref):
        koff = pl.multiple_of(c * tk, tk)
        s = lax.dot_general(k_ref[0, pl.ds(koff, tk), :], q2, _NT, preferred_element_type=F32)
        s_ref[...] = s
        part = s[:DA_SLAB]
        for r in range(DA_SLAB, tk, DA_SLAB):
            part = jnp.maximum(part, s[r:r + DA_SLAB])
        return jnp.max(part, axis=0, keepdims=True)

    def softmax(s_ref, e_ref, cmax, m):
        m_new = jnp.maximum(m, cmax)
        alpha = jnp.exp2(m - m_new)
        for r in range(0, tk, DA_SLAB):
            e_ref[r:r + DA_SLAB, :] = jnp.exp2((s_ref[r:r + DA_SLAB, :] - m_new).astype(BF16))
        return m_new, alpha

    def accumulate(acc_ref, c, e_ref, alpha):
        acc_ref[...] = alpha * acc_ref[...] + jnp.dot(vt_ref[c], e_ref[...],
                                                      preferred_element_type=F32)

    def q_blocks(it, _):
        qoffs, q2s = [], []
        for u in range(DA_STREAMS):
            qoff = pl.multiple_of((it * DA_STREAMS + u) * tq, tq)
            q = q_ref[0, pl.ds(qoff, tq), :]
            lane = lax.broadcasted_iota(jnp.int32, q.shape, 1)
            zero = jnp.zeros_like(q)
            q2s.append(jnp.concatenate([jnp.where(lane < DA_QK_DIM, q, zero),
                                        jnp.where(lane >= DA_QK_DIM, q, zero)], axis=0))
            qoffs.append(qoff)

        state = []
        for (s0, s1, e0, e1, acc), q2 in zip(streams, q2s):
            acc[...] = jnp.zeros_like(acc)
            cmax0 = scores(q2, 0, s0)
            cmax1 = scores(q2, 1, s1)
            m, alpha = softmax(s0, e0, cmax0, jnp.full((1, 2 * tq), -jnp.inf, F32))
            state += [m, cmax1, alpha]

        def pair(p, carry):
            i = 2 * p + 2
            out = []
            for u, ((s0, s1, e0, e1, acc), q2) in enumerate(zip(streams, q2s)):
                m, cmax, alpha = carry[3 * u:3 * u + 3]
                accumulate(acc, i - 2, e0, alpha)
                cmax_next = scores(q2, i, s0)
                m, alpha = softmax(s1, e1, cmax, m)
                accumulate(acc, i - 1, e1, alpha)
                cmax = scores(q2, i + 1, s1)
                m, alpha = softmax(s0, e0, cmax_next, m)
                out += [m, cmax, alpha]
            return tuple(out)

        state = lax.fori_loop(0, (n_chunks - 2) // 2, pair, tuple(state))

        for u, ((s0, s1, e0, e1, acc), qoff) in enumerate(zip(streams, qoffs)):
            m, cmax, alpha = state[3 * u:3 * u + 3]
            m, alpha_last = softmax(s1, e1, cmax, m)
            accumulate(acc, n_chunks - 2, e0, alpha)
            accumulate(acc, n_chunks - 1, e1, alpha_last)
            on = acc[:HEAD_DIM, :] / acc[HEAD_DIM:HEAD_DIM + 1, :]
            o = on[:, :tq] - lam * on[:, tq:]
            y = o * lax.rsqrt(jnp.mean(o * o, axis=0, keepdims=True) + EPS)
            y = (y * sg) * (1.0 - lam_init)
            o_ref[0, pl.ds(qoff, tq), :] = (y.T * g_ref[0, pl.ds(qoff, tq), :]).astype(o_ref.dtype)
        return 0

    lax.fori_loop(0, seq // (tq * DA_STREAMS), q_blocks, 0)


def _differential_attention(qk, v_arr, gate, lq1, lk1, lq2, lk2, subln_g, lam_init, n_heads,
                            q_blk, k_blk, v_blk, g_blk, tq=256):
    b, s, _ = qk.shape
    tq = min(tq, s // DA_STREAMS)
    tk = min(1024 if s >= 8192 else 512, s // 2)
    assert (s // tk) % 2 == 0 and s % (tq * DA_STREAMS) == 0
    small = lambda bi, h: (0, 0)
    vec = lambda a: a.reshape(1, -1).astype(F32)
    head = lambda blk: pl.BlockSpec((1, s, HEAD_DIM), lambda bi, h: (bi, 0, blk + h))
    acc_rows = HEAD_DIM + DA_ONES_ROWS
    stream_scratch = [pltpu.VMEM((tk, 2 * tq), F32),
                      pltpu.VMEM((tk, 2 * tq), F32),
                      pltpu.VMEM((tk, 2 * tq), BF16),
                      pltpu.VMEM((tk, 2 * tq), BF16),
                      pltpu.VMEM((acc_rows, 2 * tq), F32)]
    return pl.pallas_call(
        functools.partial(_da_kernel, tq=tq, tk=tk, lam_init=lam_init),
        out_shape=jax.ShapeDtypeStruct((b, s, n_heads * HEAD_DIM), BF16),
        grid=(b, n_heads),
        in_specs=[
            pl.BlockSpec((1, DA_QK_DIM), small), pl.BlockSpec((1, DA_QK_DIM), small),
            pl.BlockSpec((1, DA_QK_DIM), small), pl.BlockSpec((1, DA_QK_DIM), small),
            pl.BlockSpec((HEAD_DIM, 1), small),
            head(q_blk), head(k_blk), head(v_blk), head(g_blk),
        ],
        out_specs=pl.BlockSpec((1, s, HEAD_DIM), lambda bi, h: (bi, 0, h)),
        scratch_shapes=[pltpu.VMEM((s // tk, acc_rows, tk), BF16)]
        + stream_scratch * DA_STREAMS,
        compiler_params=_params("parallel", "parallel"),
        name="differential_attention",
    )(vec(lq1), vec(lk1), vec(lq2), vec(lk2), subln_g.reshape(-1, 1).astype(F32),
      qk, qk, v_arr, gate)


def _merge_kernel(ua_ref, ub_ref, wa_ref, wb_ref, ga_ref, gb_ref, o_ref):
    pa = jnp.dot(ua_ref[...], wa_ref[...], preferred_element_type=F32)
    pb = jnp.dot(ub_ref[...], wb_ref[...], preferred_element_type=F32)
    o_ref[...] = (ga_ref[...] * pa + gb_ref[...] * pb).astype(o_ref.dtype)


def _merge(ua, ub, wa, wb, gates, tm=512, tn=1024):
    t, k = ua.shape
    n = wa.shape[1]
    tm, tn = min(tm, t), min(tn, n)
    nj = n // tn
    return pl.pallas_call(
        _merge_kernel,
        out_shape=jax.ShapeDtypeStruct((t, n), BF16),
        grid=(t // tm, nj),
        in_specs=[pl.BlockSpec((tm, k), lambda i, j: (i, 0)),
                  pl.BlockSpec((tm, k), lambda i, j: (i, 0)),
                  pl.BlockSpec((k, tn), lambda i, j: (0, j)),
                  pl.BlockSpec((k, tn), lambda i, j: (0, j)),
                  pl.BlockSpec((tm, tn), lambda i, j: (i, j)),
                  pl.BlockSpec((tm, tn), lambda i, j: (i, nj + j))],
        out_specs=pl.BlockSpec((tm, tn), lambda i, j: (i, j)),
        compiler_params=_params("parallel", "parallel"),
        name="gated_merge",
    )(ua, ub, wa, wb, gates, gates)


def _out_proj_kernel(m_ref, w_ref, x_ref, o_ref):
    o_ref[...] = x_ref[...] + jnp.dot(m_ref[...], w_ref[...], preferred_element_type=F32)


def _out_proj(merged, wo, x, tm=1024, tn=512):
    t, k = merged.shape
    n = wo.shape[1]
    tm, tn = min(tm, t), min(tn, n)
    return pl.pallas_call(
        _out_proj_kernel,
        out_shape=jax.ShapeDtypeStruct((t, n), F32),
        grid=(t // tm, n // tn),
        in_specs=[pl.BlockSpec((tm, k), lambda i, j: (i, 0)),
                  pl.BlockSpec((k, tn), lambda i, j: (0, j)),
                  pl.BlockSpec((tm, tn), lambda i, j: (i, j))],
        out_specs=pl.BlockSpec((tm, tn), lambda i, j: (i, j)),
        compiler_params=_params("parallel", "parallel"),
        name="out_proj_residual",
    )(merged, wo, x)


def _layer_weights(w_in_l, w_up_a_l, w_up_b_l, w_o_l, d_model):
    bw = d_model // 2
    col = lambda i: w_in_l[:, i * bw:(i + 1) * bw]
    ones = jnp.ones((1, bw), F32)
    return dict(
        plain=jnp.concatenate([col(0), col(1), col(2), col(6)], axis=1).astype(BF16),
        plain_scale=jnp.concatenate([ones * (HEAD_DIM ** -0.5 * LOG2E), ones, ones, ones], axis=1),
        rope=jnp.concatenate([col(4), col(5)], axis=1).astype(BF16),
        rope_scale=jnp.concatenate([ones * (DA_QK_DIM ** -0.5 * LOG2E), ones], axis=1),
        gate=jnp.concatenate([col(3), col(7)], axis=1).astype(BF16),
        merge=w_in_l[:, 8 * bw:].astype(BF16),
        up_a=w_up_a_l.astype(BF16), up_b=w_up_b_l.astype(BF16), out=w_o_l.astype(BF16),
    )


def _layer(x, batch, seq, w, g_norm, b_merge, na_tabs, rope_tabs, lam_params, subln_g, lam_init):
    t, d = x.shape
    bw = d // 2
    n_heads = bw // HEAD_DIM
    h = _rmsnorm(x, g_norm, BF16)
    tn = min(1024, bw)
    plain = _proj(_proj_scaled_kernel, h, w["plain"], BF16, extra=(w["plain_scale"],),
                  extra_specs=[_row_spec(tn)], tn=tn, name="proj_plain")
    tm = min(1024, seq)
    n_pos_blocks = seq // tm
    tab_spec = pl.BlockSpec((tm, LANES), lambda i, j: (i % n_pos_blocks, 0))
    roped = _proj(_proj_rope_kernel, h, w["rope"], BF16, extra=(w["rope_scale"],) + tuple(rope_tabs),
                  extra_specs=[_row_spec(tn)] + [tab_spec] * 3, tm=tm, tn=tn, name="proj_rope")
    silu = _proj(_proj_silu_kernel, h, w["gate"], F32, tn=tn, name="proj_silu")
    gates = _proj(_proj_sigmoid_kernel, h, w["merge"], F32, extra=(b_merge.reshape(1, -1),),
                  extra_specs=[_row_spec(tn)], tn=tn, name="proj_sigmoid")

    plain3 = plain.reshape(batch, seq, 4 * bw)
    silu3 = silu.reshape(batch, seq, 2 * bw)
    ua = _neighbourhood_attention(plain3, silu3, na_tabs, n_heads,
                                  q_blk=0, k_blk=n_heads, v_blk=2 * n_heads, g_blk=0)
    ub = _differential_attention(roped.reshape(batch, seq, 2 * bw), plain3, silu3, *lam_params,
                                 subln_g, lam_init, n_heads,
                                 q_blk=0, k_blk=n_heads, v_blk=3 * n_heads, g_blk=n_heads)
    merged = _merge(ua.reshape(t, bw), ub.reshape(t, bw), w["up_a"], w["up_b"], gates)
    return _out_proj(merged, w["out"], x)


def kernel(x_prompt, x_sample, norm_g, w_in, b_merge, rpb, lambda_q1, lambda_k1, lambda_q2,
           lambda_k2, subln_g, w_up_a, w_up_b, w_o, final_g):
    depth = norm_g.shape[0]
    d_model = x_prompt.shape[-1]
    weights = [_layer_weights(w_in[l], w_up_a[l], w_up_b[l], w_o[l], d_model) for l in range(depth)]
    outs = []
    for x in (x_prompt, x_sample):
        batch, seq, _ = x.shape
        rope_tabs = _rope_tables(seq)
        xt = x.reshape(batch * seq, d_model)
        for l in range(depth):
            lam_init = 0.8 - 0.6 * math.exp(-0.3 * l)
            na_tabs = _na_bias_tables(rpb[l], seq // GRID_W)
            lam_params = (lambda_q1[l], lambda_k1[l], lambda_q2[l], lambda_k2[l])
            xt = _layer(xt, batch, seq, weights[l], norm_g[l], b_merge[l], na_tabs, rope_tabs,
                        lam_params, subln_g[l], lam_init)
        outs.append(_rmsnorm(xt, final_g, F32).reshape(batch, seq, d_model))
    return tuple(outs)
```
